```python
import math
import jax, jax.numpy as jnp
from jax import lax
import numpy as np

D_MODEL = 2048
BATCH = 8
SEQ = 2048
DEPTH = 1
DEC_BATCH = 128
DEC_SEQ = 8
PAST_LEN = 8192
PAGE_SIZE = 128

RET_HEADS = 8
RET_DK = 128
RET_DV = 256
RET_CHUNK = 128
SWA_Q_HEADS = 32
SWA_KV_HEADS = 4
SWA_GROUP = SWA_Q_HEADS // SWA_KV_HEADS
SWA_DH = 64
WINDOW = 128
SWA_BLOCK = 128
N_EXPERTS = 32
TOP_K = 4
D_FF = D_MODEL
SWIGLU_LIMIT = 7.0
SWIGLU_ALPHA = 1.702
NORM_EPS = 1e-5
N_MOD = 6
IN_SIZES = (RET_HEADS * RET_DK, RET_HEADS * RET_DK, RET_HEADS * RET_DV, RET_HEADS * RET_DV,
            SWA_Q_HEADS * SWA_DH, SWA_KV_HEADS * SWA_DH, SWA_KV_HEADS * SWA_DH, D_MODEL, D_MODEL)
D_IN = sum(IN_SIZES)

kernel_name = "retention_swa_sink_moe_adaln_step"


def _rmsnorm(x, g):
    xf = x.astype(jnp.float32)
    y = xf * lax.rsqrt(jnp.mean(xf * xf, axis=-1, keepdims=True) + NORM_EPS)
    return (y * g.astype(jnp.float32)).astype(x.dtype)


def _modulate(x, g, shift, scale):
    return _rmsnorm(x, g) * (1.0 + scale[:, None, :]) + shift[:, None, :]


def _split_in(proj):
    outs, start = [], 0
    for n in IN_SIZES:
        outs.append(proj[..., start:start + n])
        start += n
    return outs


def _ret_log_decay():
    return jnp.log(1.0 - 2.0 ** (-5.0 - jnp.arange(RET_HEADS, dtype=jnp.float32)))


def _alibi_slopes():
    h = jnp.arange(1, SWA_Q_HEADS + 1, dtype=jnp.float32)
    return (2.0 ** (-8.0 * h / SWA_Q_HEADS)).reshape(SWA_KV_HEADS, SWA_GROUP)


def _retention_chunk(S, q, k, v, log_g):
    L = q.shape[1]
    pos = jnp.arange(L, dtype=jnp.float32)
    diff = pos[:, None] - pos[None, :]
    decay = jnp.where((diff >= 0)[None], jnp.exp(jnp.maximum(diff, 0.0)[None] * log_g[:, None, None]), 0.0)
    scores = jnp.einsum('bihd,bjhd->bhij', q, k) * decay[None]
    inner = jnp.einsum('bhij,bjhe->bihe', scores, v)
    q_dec = jnp.exp((pos + 1.0)[:, None] * log_g[None, :])
    cross = jnp.einsum('bihd,bhde->bihe', q * q_dec[None, :, :, None], S)
    k_dec = jnp.exp((L - 1.0 - pos)[:, None] * log_g[None, :])
    S_new = jnp.exp(L * log_g)[None, :, None, None] * S + jnp.einsum('bjhd,bjhe->bhde', k * k_dec[None, :, :, None], v)
    return S_new, inner + cross


def _retention_prompt(q, k, v, log_g):
    B, T = q.shape[:2]
    nc = T // RET_CHUNK

    def to_chunks(a):
        return jnp.moveaxis(a.reshape(B, nc, RET_CHUNK, *a.shape[2:]), 1, 0)

    S0 = jnp.zeros((B, RET_HEADS, RET_DK, RET_DV), jnp.float32)

    def step(S, qkv):
        qc, kc, vc = qkv
        return _retention_chunk(S, qc, kc, vc, log_g)

    S_fin, o = lax.scan(step, S0, (to_chunks(q), to_chunks(k), to_chunks(v)))
    o = jnp.moveaxis(o, 0, 1).reshape(B, T, RET_HEADS, RET_DV)
    return o, S_fin


def _retention_out(o, g_lin, gn_g, dtype):
    o = o * lax.rsqrt(jnp.mean(o * o, axis=-1, keepdims=True) + NORM_EPS)
    o = o.reshape(o.shape[0], o.shape[1], RET_HEADS * RET_DV) * gn_g.astype(jnp.float32)
    return (jax.nn.silu(g_lin.astype(jnp.float32)) * o).astype(dtype)


def _sink_attention(q, k, v, q_pos, k_pos, k_valid, sinks, slopes):
    s = jnp.einsum('...qhgd,...khd->...hgqk', q, k).astype(jnp.float32) * (SWA_DH ** -0.5)
    dist = q_pos[..., :, None] - k_pos[..., None, :]
    ok = (dist >= 0) & (dist < WINDOW) & k_valid[..., None, :]
    s = s - slopes[:, :, None, None] * dist.astype(jnp.float32)[..., None, None, :, :]
    s = jnp.where(ok[..., None, None, :, :], s, -jnp.inf)
    sink = jnp.broadcast_to(sinks.astype(jnp.float32)[:, :, None, None], s.shape[:-1] + (1,))
    p = jax.nn.softmax(jnp.concatenate([s, sink], axis=-1), axis=-1)[..., :-1]
    return jnp.einsum('...hgqk,...khd->...qhgd', p.astype(v.dtype), v)


def _swa_prompt(q, k, v, sinks, slopes):
    B, T = q.shape[:2]
    nb = T // SWA_BLOCK
    qb = q.reshape(B, nb, SWA_BLOCK, SWA_KV_HEADS, SWA_GROUP, SWA_DH)

    def band(a):
        prev = jnp.pad(a, ((0, 0), (SWA_BLOCK, 0), (0, 0), (0, 0)))[:, :T]
        return jnp.concatenate([prev.reshape(B, nb, SWA_BLOCK, SWA_KV_HEADS, SWA_DH),
                                a.reshape(B, nb, SWA_BLOCK, SWA_KV_HEADS, SWA_DH)], axis=2)

    start = jnp.arange(nb, dtype=jnp.int32)[:, None] * SWA_BLOCK
    q_pos = start + jnp.arange(SWA_BLOCK, dtype=jnp.int32)[None, :]
    k_pos = start - SWA_BLOCK + jnp.arange(2 * SWA_BLOCK, dtype=jnp.int32)[None, :]
    o = _sink_attention(qb, band(k), band(v), q_pos, k_pos, k_pos >= 0, sinks, slopes)
    return o.reshape(B, T, SWA_Q_HEADS * SWA_DH)


def _swa_sample(q, k_new, v_new, k_buf, v_buf, sinks, slopes):
    Bd, L = q.shape[:2]
    W = k_buf.shape[1]
    k_all = jnp.concatenate([k_buf, k_new.astype(k_buf.dtype)], axis=1)
    v_all = jnp.concatenate([v_buf, v_new.astype(v_buf.dtype)], axis=1)
    q_pos = PAST_LEN + jnp.arange(L, dtype=jnp.int32)
    k_pos = PAST_LEN - W + jnp.arange(W + L, dtype=jnp.int32)
    qg = q.reshape(Bd, L, SWA_KV_HEADS, SWA_GROUP, SWA_DH)
    o = _sink_attention(qg, k_all, v_all.astype(q.dtype), q_pos, k_pos, jnp.ones((W + L,), bool), sinks, slopes)
    return o.reshape(Bd, L, SWA_Q_HEADS * SWA_DH), k_all[:, -W:], v_all[:, -W:]


def _moe(h, w_router, b_router, w_gate_up, b_gate_up, w_down, b_down):
    T = h.shape[0]
    logits = (h @ w_router).astype(jnp.float32) + b_router.astype(jnp.float32)
    top_val, top_idx = lax.top_k(logits, TOP_K)
    probs = jax.nn.softmax(top_val, axis=-1)
    flat_e = top_idx.reshape(-1)
    order = jnp.argsort(flat_e)
    e_sorted = flat_e[order]
    tok = order // TOP_K
    sizes = jnp.bincount(flat_e, length=N_EXPERTS).astype(jnp.int32)
    xs = h[tok]
    gu = lax.ragged_dot(xs, w_gate_up, sizes) + b_gate_up[e_sorted]
    gate = jnp.minimum(gu[:, :D_FF], SWIGLU_LIMIT)
    up = jnp.clip(gu[:, D_FF:], -SWIGLU_LIMIT, SWIGLU_LIMIT)
    act = (up + 1.0) * gate * jax.nn.sigmoid(SWIGLU_ALPHA * gate)
    out = lax.ragged_dot(act, w_down, sizes) + b_down[e_sorted]
    out = out * probs.reshape(-1)[order][:, None].astype(out.dtype)
    return jnp.zeros((T, h.shape[1]), out.dtype).at[tok].add(out)


def _layer(x, c, ret_state, k_buf, v_buf, w_ada, b_ada, norm_mix_g, w_in, ret_gn_g, w_ret_o, swa_sinks,
           w_swa_o, w_out, norm_ffn_g, w_router, b_router, w_gate_up, b_gate_up, w_down, b_down):
    B, L, _ = x.shape
    f32 = jnp.float32
    mod = (c @ w_ada + b_ada).reshape(B, N_MOD, D_MODEL)
    sh1, sc1, g1, sh2, sc2, g2 = [mod[:, i] for i in range(N_MOD)]

    h = _modulate(x, norm_mix_g, sh1, sc1)
    rq, rk, rv, rg, sq, sk, sv, ga, gb = _split_in(h @ w_in)
    q = rq.reshape(B, L, RET_HEADS, RET_DK).astype(f32)
    k = rk.reshape(B, L, RET_HEADS, RET_DK).astype(f32) * (RET_DK ** -0.5)
    v = rv.reshape(B, L, RET_HEADS, RET_DV).astype(f32)
    log_g = _ret_log_decay()
    sq = sq.reshape(B, L, SWA_Q_HEADS, SWA_DH)
    sk = sk.reshape(B, L, SWA_KV_HEADS, SWA_DH)
    sv = sv.reshape(B, L, SWA_KV_HEADS, SWA_DH)
    sinks = swa_sinks.reshape(SWA_KV_HEADS, SWA_GROUP)
    slopes = _alibi_slopes()

    if ret_state is None:
        o_ret, S_new = _retention_prompt(q, k, v, log_g)
        o_swa = _swa_prompt(sq, sk, sv, sinks, slopes)
        W = min(WINDOW, L)
        k_keep, v_keep = sk[:, L - W:], sv[:, L - W:]
        S_new = S_new.astype(x.dtype)
    else:
        S_new, o_ret = _retention_chunk(ret_state.astype(f32), q, k, v, log_g)
        o_swa, k_keep, v_keep = _swa_sample(sq, sk, sv, k_buf, v_buf, sinks, slopes)
        S_new = S_new.astype(ret_state.dtype)

    a_out = _retention_out(o_ret, rg, ret_gn_g, x.dtype) @ w_ret_o
    b_out = o_swa @ w_swa_o
    merged = jax.nn.sigmoid(ga) * a_out + jax.nn.sigmoid(gb) * b_out
    x = x + g1[:, None, :] * (merged @ w_out)

    h2 = _modulate(x, norm_ffn_g, sh2, sc2)
    ffn = _moe(h2.reshape(B * L, D_MODEL), w_router, b_router, w_gate_up, b_gate_up, w_down, b_down)
    x = x + g2[:, None, :] * ffn.reshape(B, L, D_MODEL)
    return x, S_new, k_keep, v_keep


def setup_inputs(seed: int = 0) -> dict:
    key = jax.random.key(seed)
    ks = jax.random.split(key, 24)
    f32 = jnp.float32

    def nrm(k, shape, s):
        return jax.random.normal(k, shape, f32) * s

    wb = min(WINDOW, PAST_LEN)
    return {
        "x_prompt": nrm(ks[0], (BATCH, SEQ, D_MODEL), 1.0),
        "x_sample": nrm(ks[1], (DEC_BATCH, DEC_SEQ, D_MODEL), 1.0),
        "state_ret": nrm(ks[2], (DEPTH, DEC_BATCH, RET_HEADS, RET_DK, RET_DV), 1.0),
        "cache_swa_k": nrm(ks[3], (DEPTH, DEC_BATCH, wb, SWA_KV_HEADS, SWA_DH), 1.0),
        "cache_swa_v": nrm(ks[4], (DEPTH, DEC_BATCH, wb, SWA_KV_HEADS, SWA_DH), 1.0),
        "c_prompt": nrm(ks[5], (BATCH, D_MODEL), 1.0),
        "c_sample": nrm(ks[6], (DEC_BATCH, D_MODEL), 1.0),
        "w_ada": nrm(ks[7], (DEPTH, D_MODEL, N_MOD * D_MODEL), 0.3 * D_MODEL ** -0.5),
        "b_ada": nrm(ks[8], (DEPTH, N_MOD * D_MODEL), 0.02),
        "norm_mix_g": 1.0 + nrm(ks[9], (DEPTH, D_MODEL), 0.02),
        "w_in": nrm(ks[10], (DEPTH, D_MODEL, D_IN), D_MODEL ** -0.5),
        "ret_gn_g": 1.0 + nrm(ks[11], (DEPTH, RET_HEADS * RET_DV), 0.02),
        "w_ret_o": nrm(ks[12], (DEPTH, RET_HEADS * RET_DV, D_MODEL), (RET_HEADS * RET_DV) ** -0.5),
        "swa_sinks": nrm(ks[13], (DEPTH, SWA_Q_HEADS), 1.0),
        "w_swa_o": nrm(ks[14], (DEPTH, SWA_Q_HEADS * SWA_DH, D_MODEL), (SWA_Q_HEADS * SWA_DH) ** -0.5),
        "w_out": nrm(ks[15], (DEPTH, D_MODEL, D_MODEL), D_MODEL ** -0.5),
        "norm_ffn_g": 1.0 + nrm(ks[16], (DEPTH, D_MODEL), 0.02),
        "w_router": nrm(ks[17], (DEPTH, D_MODEL, N_EXPERTS), D_MODEL ** -0.5),
        "b_router": nrm(ks[18], (DEPTH, N_EXPERTS), 0.01),
        "w_gate_up": nrm(ks[19], (DEPTH, N_EXPERTS, D_MODEL, 2 * D_FF), D_MODEL ** -0.5),
        "b_gate_up": nrm(ks[20], (DEPTH, N_EXPERTS, 2 * D_FF), 0.02),
        "w_down": nrm(ks[21], (DEPTH, N_EXPERTS, D_FF, D_MODEL), D_FF ** -0.5),
        "b_down": nrm(ks[22], (DEPTH, N_EXPERTS, D_MODEL), 0.02),
        "norm_final_g": 1.0 + nrm(ks[23], (D_MODEL,), 0.02),
    }


def reference(x_prompt, x_sample, state_ret, cache_swa_k, cache_swa_v, c_prompt, c_sample, w_ada, b_ada,
              norm_mix_g, w_in, ret_gn_g, w_ret_o, swa_sinks, w_swa_o, w_out, norm_ffn_g, w_router, b_router,
              w_gate_up, b_gate_up, w_down, b_down, norm_final_g):
    yp, ys = x_prompt, x_sample
    ret_p, k_p, v_p, ret_s, k_s, v_s = [], [], [], [], [], []
    for l in range(DEPTH):
        w = (w_ada[l], b_ada[l], norm_mix_g[l], w_in[l], ret_gn_g[l], w_ret_o[l], swa_sinks[l], w_swa_o[l],
             w_out[l], norm_ffn_g[l], w_router[l], b_router[l], w_gate_up[l], b_gate_up[l], w_down[l], b_down[l])
        yp, sp, kp, vp = _layer(yp, c_prompt, None, None, None, *w)
        ys, ss, kss, vss = _layer(ys, c_sample, state_ret[l], cache_swa_k[l], cache_swa_v[l], *w)
        ret_p.append(sp)
        k_p.append(kp)
        v_p.append(vp)
        ret_s.append(ss)
        k_s.append(kss)
        v_s.append(vss)
    y_prompt = _rmsnorm(yp, norm_final_g)
    y_sample = _rmsnorm(ys, norm_final_g)
    return (y_prompt, y_sample, jnp.stack(ret_p), jnp.stack(k_p), jnp.stack(v_p),
            jnp.stack(ret_s), jnp.stack(k_s), jnp.stack(v_s))
```

```python
import functools
import math
from typing import NamedTuple

import numpy as np
import jax
import jax.numpy as jnp
from jax import lax
from jax.experimental import pallas as pl
from jax.experimental.pallas import tpu as pltpu

F32 = jnp.float32
BF16 = jnp.bfloat16
I32 = jnp.int32

NORM_EPS = 1e-5
SWIGLU_LIMIT = 7.0
SWIGLU_ALPHA = 1.702
N_MOD = 6
NEG_BIG = -1e30
LANES = 128
MOE_CHUNK = 1024
MOE_SUB = 256
VMEM_LIMIT = 56 * 1024 * 1024


class Cfg(NamedTuple):
    d_model: int = 2048
    ret_heads: int = 8
    ret_dk: int = 128
    ret_dv: int = 256
    chunk: int = 128
    q_heads: int = 32
    kv_heads: int = 4
    dh: int = 64
    window: int = 128
    n_experts: int = 32
    top_k: int = 4
    d_ff: int = 2048
    past_len: int = 8192


def _offsets(cfg):
    sizes = (cfg.ret_heads * cfg.ret_dk, cfg.ret_heads * cfg.ret_dk, cfg.ret_heads * cfg.ret_dv,
             cfg.ret_heads * cfg.ret_dv, cfg.q_heads * cfg.dh, cfg.kv_heads * cfg.dh, cfg.kv_heads * cfg.dh,
             cfg.d_model, cfg.d_model)
    offs = np.concatenate([[0], np.cumsum(sizes)]).astype(int)
    names = ("rq", "rk", "rv", "rg", "sq", "sk", "sv", "ga", "gb")
    return {n: int(o) for n, o in zip(names, offs[:-1])}, int(offs[-1])


def _ret_log_decay(cfg):
    h = np.arange(cfg.ret_heads, dtype=np.float32)
    return [float(v) for v in np.log((1.0 - np.float32(2.0) ** (-5.0 - h)).astype(np.float32)).astype(np.float32)]


def _alibi_slope(head, n_heads):
    return float(np.float32(2.0) ** np.float32(-8.0 * (head + 1) / n_heads))


def _sigmoid(x):
    return 1.0 / (1.0 + jnp.exp(-x))


def _cparams(sem, vmem=None):
    return pltpu.CompilerParams(dimension_semantics=sem, vmem_limit_bytes=vmem)


def _ada_kernel(c_ref, w_ref, b_ref, o_ref):
    o_ref[...] = jnp.dot(c_ref[...].astype(BF16), w_ref[...].astype(BF16),
                         preferred_element_type=F32) + b_ref[...]


def _adaln(c_all, w_ada, b_ada):
    r, d = c_all.shape
    n = w_ada.shape[1]
    tn = min(1024, d)
    assert n % tn == 0
    return pl.pallas_call(
        _ada_kernel,
        grid=(n // tn,),
        in_specs=[pl.BlockSpec((r, d), lambda j: (0, 0)),
                  pl.BlockSpec((d, tn), lambda j: (0, j)),
                  pl.BlockSpec((1, tn), lambda j: (0, j))],
        out_specs=pl.BlockSpec((r, tn), lambda j: (0, j)),
        out_shape=jax.ShapeDtypeStruct((r, n), F32),
        compiler_params=_cparams(("arbitrary",), VMEM_LIMIT),
        name="adaln",
    )(c_all, w_ada, b_ada.reshape(1, n))


def _mod_rows(ref, per_token):
    return ref[...] if per_token else ref[0]


def _modulated_norm(x, g, sh, sc):
    y = x * lax.rsqrt(jnp.mean(x * x, axis=-1, keepdims=True) + NORM_EPS) * g
    return y * (1.0 + sc) + sh


def _inproj_kernel(x_ref, sh_ref, sc_ref, g_ref, w_ref, proj_ref, kv_ref, h_scr, *, per_token, kv_tile):
    j = pl.program_id(1)

    @pl.when(j == 0)
    def _():
        h = _modulated_norm(x_ref[...], g_ref[...], _mod_rows(sh_ref, per_token), _mod_rows(sc_ref, per_token))
        h_scr[...] = h.astype(BF16)

    acc = jnp.dot(h_scr[...], w_ref[...], preferred_element_type=F32)
    proj_ref[...] = acc.astype(proj_ref.dtype)

    @pl.when(j == kv_tile)
    def _():
        kv_ref[...] = acc


def _mod_spec(per_token, tm, d, rows_per_tile_inv):
    if per_token:
        return pl.BlockSpec((tm, d), lambda i, *_: (i, 0))
    return pl.BlockSpec((1, 1, d), lambda i, *_: (i // rows_per_tile_inv, 0, 0))


def _inproj(cfg, x, sh, sc, g, w_in_b, per_token, tm, tiles_per_seq):
    t, d = x.shape
    offs, d_in = _offsets(cfg)
    tn = 2 * cfg.kv_heads * cfg.dh
    assert offs["sk"] % tn == 0 and d_in % tn == 0 and t % tm == 0
    kern = functools.partial(_inproj_kernel, per_token=per_token, kv_tile=offs["sk"] // tn)
    return pl.pallas_call(
        kern,
        grid=(t // tm, d_in // tn),
        in_specs=[pl.BlockSpec((tm, d), lambda i, j: (i, 0)),
                  _mod_spec(per_token, tm, d, tiles_per_seq),
                  _mod_spec(per_token, tm, d, tiles_per_seq),
                  pl.BlockSpec((1, d), lambda i, j: (0, 0)),
                  pl.BlockSpec((d, tn), lambda i, j: (0, j))],
        out_specs=[pl.BlockSpec((tm, tn), lambda i, j: (i, j)),
                   pl.BlockSpec((tm, tn), lambda i, j: (i, 0))],
        out_shape=[jax.ShapeDtypeStruct((t, d_in), BF16), jax.ShapeDtypeStruct((t, tn), F32)],
        scratch_shapes=[pltpu.VMEM((tm, d), BF16)],
        compiler_params=_cparams(("parallel", "arbitrary"), VMEM_LIMIT),
        name="inproj",
    )(x, sh, sc, g.reshape(1, d), w_in_b)


def _ret_head(q, k, v, s_prev, lg, l_eff, kscale):
    r = q.shape[0]
    ii = lax.broadcasted_iota(I32, (r, r), 0)
    jj = lax.broadcasted_iota(I32, (r, r), 1)
    diff = (ii - jj).astype(F32)
    decay = jnp.where(diff >= 0.0, jnp.exp(jnp.maximum(diff, 0.0) * lg), 0.0) * kscale
    pos = lax.broadcasted_iota(I32, (r, 1), 0).astype(F32)
    scores = lax.dot_general(q, k, (((1,), (1,)), ((), ())), preferred_element_type=F32) * decay
    inner = jnp.dot(scores.astype(BF16), v, preferred_element_type=F32)
    cross = jnp.dot(q, s_prev.astype(BF16), preferred_element_type=F32) * jnp.exp((pos + 1.0) * lg)
    k_dec = jnp.exp((l_eff - 1.0 - pos) * lg) * kscale
    kd_t = (k.astype(F32) * k_dec).T.astype(BF16)
    s_new = math.exp(l_eff * lg) * s_prev + jnp.dot(kd_t, v, preferred_element_type=F32)
    return inner + cross, s_new


def _ret_out(o, gate, gn):
    on = o * lax.rsqrt(jnp.mean(o * o, axis=-1, keepdims=True) + NORM_EPS) * gn
    return gate * _sigmoid(gate) * on


def _ret_prompt_kernel(q_ref, k_ref, v_ref, rg_ref, gn_ref, o_ref, s_out_ref, s_scr, *, cfg, log_g):
    c = pl.program_id(1)
    dk, dv = cfg.ret_dk, cfg.ret_dv

    @pl.when(c == 0)
    def _():
        s_scr[...] = jnp.zeros_like(s_scr)

    for h in range(cfg.ret_heads):
        q = q_ref[:, h * dk:(h + 1) * dk]
        k = k_ref[:, h * dk:(h + 1) * dk]
        v = v_ref[:, h * dv:(h + 1) * dv]
        o, s_new = _ret_head(q, k, v, s_scr[h], log_g[h], float(cfg.chunk), dk ** -0.5)
        s_scr[h] = s_new
        gate = rg_ref[:, h * dv:(h + 1) * dv].astype(F32)
        o_ref[:, h * dv:(h + 1) * dv] = _ret_out(o, gate, gn_ref[:, h * dv:(h + 1) * dv]).astype(o_ref.dtype)

    @pl.when(c == pl.num_programs(1) - 1)
    def _():
        s_out_ref[0] = s_scr[...]


def _ret_prompt(cfg, proj, gn_g, batch, seq):
    t = proj.shape[0]
    hk, hv = cfg.ret_heads * cfg.ret_dk, cfg.ret_heads * cfg.ret_dv
    assert cfg.ret_dv == 2 * cfg.ret_dk
    nc = seq // cfg.chunk
    l = cfg.chunk
    kern = functools.partial(_ret_prompt_kernel, cfg=cfg, log_g=_ret_log_decay(cfg))
    return pl.pallas_call(
        kern,
        grid=(batch, nc),
        in_specs=[pl.BlockSpec((l, hk), lambda b, c: (b * nc + c, 0)),
                  pl.BlockSpec((l, hk), lambda b, c: (b * nc + c, 1)),
                  pl.BlockSpec((l, hv), lambda b, c: (b * nc + c, 1)),
                  pl.BlockSpec((l, hv), lambda b, c: (b * nc + c, 2)),
                  pl.BlockSpec((1, hv), lambda b, c: (0, 0))],
        out_specs=[pl.BlockSpec((l, hv), lambda b, c: (b * nc + c, 0)),
                   pl.BlockSpec((1, cfg.ret_heads, cfg.ret_dk, cfg.ret_dv), lambda b, c: (b, 0, 0, 0))],
        out_shape=[jax.ShapeDtypeStruct((t, hv), BF16),
                   jax.ShapeDtypeStruct((batch, cfg.ret_heads, cfg.ret_dk, cfg.ret_dv), F32)],
        scratch_shapes=[pltpu.VMEM((cfg.ret_heads, cfg.ret_dk, cfg.ret_dv), F32)],
        compiler_params=_cparams(("parallel", "arbitrary"), VMEM_LIMIT),
        name="ret_prompt",
    )(proj, proj, proj, proj, gn_g.reshape(1, hv))


def _ret_sample_kernel(q_ref, k_ref, v_ref, rg_ref, gn_ref, s_in_ref, o_ref, s_out_ref, *, cfg, log_g, group, l):
    dk, dv = cfg.ret_dk, cfg.ret_dv
    rpad = cfg.chunk
    qf = q_ref[...].astype(F32)
    kf = k_ref[...].astype(F32)
    vf = v_ref[...].astype(F32)
    gf = rg_ref[...].astype(F32)

    def pad(a):
        return jnp.concatenate([a, jnp.zeros((rpad - l, a.shape[1]), F32)], axis=0).astype(BF16)

    for s in range(group):
        rows = slice(s * l, (s + 1) * l)
        for h in range(cfg.ret_heads):
            q = pad(qf[rows, h * dk:(h + 1) * dk])
            k = pad(kf[rows, h * dk:(h + 1) * dk])
            v = pad(vf[rows, h * dv:(h + 1) * dv])
            o, s_new = _ret_head(q, k, v, s_in_ref[s, h], log_g[h], float(l), dk ** -0.5)
            s_out_ref[s, h] = s_new
            o_ref[rows, h * dv:(h + 1) * dv] = _ret_out(o[:l], gf[rows, h * dv:(h + 1) * dv],
                                                        gn_ref[:, h * dv:(h + 1) * dv])


def _ret_sample(cfg, proj, gn_g, state, batch, l, group=4):
    t = proj.shape[0]
    hk, hv = cfg.ret_heads * cfg.ret_dk, cfg.ret_heads * cfg.ret_dv
    rows = group * l
    kern = functools.partial(_ret_sample_kernel, cfg=cfg, log_g=_ret_log_decay(cfg), group=group, l=l)
    sblk = (group, cfg.ret_heads, cfg.ret_dk, cfg.ret_dv)
    return pl.pallas_call(
        kern,
        grid=(batch // group,),
        in_specs=[pl.BlockSpec((rows, hk), lambda i: (i, 0)),
                  pl.BlockSpec((rows, hk), lambda i: (i, 1)),
                  pl.BlockSpec((rows, hv), lambda i: (i, 1)),
                  pl.BlockSpec((rows, hv), lambda i: (i, 2)),
                  pl.BlockSpec((1, hv), lambda i: (0, 0)),
                  pl.BlockSpec(sblk, lambda i: (i, 0, 0, 0))],
        out_specs=[pl.BlockSpec((rows, hv), lambda i: (i, 0)),
                   pl.BlockSpec(sblk, lambda i: (i, 0, 0, 0))],
        out_shape=[jax.ShapeDtypeStruct((t, hv), F32),
                   jax.ShapeDtypeStruct(state.shape, F32)],
        compiler_params=_cparams(("parallel",), VMEM_LIMIT),
        name="ret_sample",
    )(proj, proj, proj, proj, gn_g.reshape(1, hv), state)


def _softmax_sink(s, sink):
    m = jnp.maximum(jnp.max(s, axis=-1, keepdims=True), sink)
    p = jnp.exp(s - m)
    den = jnp.sum(p, axis=-1, keepdims=True) + jnp.exp(sink - m)
    return p, 1.0 / den


def _swa_prompt_kernel(sink_ref, q_ref, cur_ref, prev_ref, o_ref, *, cfg):
    jb = pl.program_id(1)
    blk = cfg.window
    dh = cfg.dh
    kvw = cfg.kv_heads * dh
    grp = cfg.q_heads // cfg.kv_heads
    half = LANES // 2
    assert dh == half and grp % 2 == 0
    scale = dh ** -0.5

    ii = lax.broadcasted_iota(I32, (blk, 2 * blk), 0)
    mm = lax.broadcasted_iota(I32, (blk, 2 * blk), 1)
    dist = ii + blk - mm
    first_key = jnp.where(jb > 0, 0, blk)
    ok = (dist >= 0) & (dist < cfg.window) & (mm >= first_key)
    distf = dist.astype(F32)
    negmask = jnp.where(ok, 0.0, NEG_BIG)
    lane = lax.broadcasted_iota(I32, (2 * blk, LANES), 1)
    lo = lane < half
    lane_o = lax.broadcasted_iota(I32, (blk, LANES), 1) < half

    for g in range(cfg.kv_heads):
        slab = g // 2
        ks = jnp.concatenate([prev_ref[:, slab * LANES:(slab + 1) * LANES],
                              cur_ref[:, slab * LANES:(slab + 1) * LANES]], axis=0)
        vs = jnp.concatenate([prev_ref[:, kvw + slab * LANES:kvw + (slab + 1) * LANES],
                              cur_ref[:, kvw + slab * LANES:kvw + (slab + 1) * LANES]], axis=0)
        ks_r = pltpu.roll(ks, half, 1)
        vs_r = pltpu.roll(vs, half, 1)
        if g % 2 == 0:
            k_lo, k_hi, v_lo, v_hi = ks, ks_r, vs, vs_r
        else:
            k_lo, k_hi, v_lo, v_hi = ks_r, ks, vs_r, vs
        k_lo = jnp.where(lo, k_lo, 0.0).astype(BF16)
        k_hi = jnp.where(lo, 0.0, k_hi).astype(BF16)
        v_lo = jnp.where(lo, v_lo, 0.0).astype(BF16)
        v_hi = jnp.where(lo, 0.0, v_hi).astype(BF16)
        for p in range(grp // 2):
            ha = g * grp + 2 * p
            qp = q_ref[:, ha * dh:ha * dh + LANES]
            outs = []
            for head, kk, vv in ((ha, k_lo, v_lo), (ha + 1, k_hi, v_hi)):
                s = lax.dot_general(qp, kk, (((1,), (1,)), ((), ())), preferred_element_type=F32)
                s = s * scale - _alibi_slope(head, cfg.q_heads) * distf + negmask
                pr, inv = _softmax_sink(s, sink_ref[head])
                outs.append((jnp.dot(pr.astype(BF16), vv, preferred_element_type=F32), inv))
            (oa, ia), (ob, ib) = outs
            o_ref[:, ha * dh:ha * dh + LANES] = ((oa + ob) * jnp.where(lane_o, ia, ib)).astype(o_ref.dtype)


def _swa_prompt(cfg, proj, kv, sinks, batch, seq):
    t = proj.shape[0]
    offs, _ = _offsets(cfg)
    qw = cfg.q_heads * cfg.dh
    kvw2 = 2 * cfg.kv_heads * cfg.dh
    blk = cfg.window
    nb = seq // blk
    assert offs["sq"] % qw == 0
    qblk = offs["sq"] // qw
    kern = functools.partial(_swa_prompt_kernel, cfg=cfg)
    return pl.pallas_call(
        kern,
        grid_spec=pltpu.PrefetchScalarGridSpec(
            num_scalar_prefetch=1,
            grid=(batch, nb),
            in_specs=[pl.BlockSpec((blk, qw), lambda b, j, s: (b * nb + j, qblk)),
                      pl.BlockSpec((blk, kvw2), lambda b, j, s: (b * nb + j, 0)),
                      pl.BlockSpec((blk, kvw2), lambda b, j, s: (b * nb + jnp.maximum(j - 1, 0), 0))],
            out_specs=pl.BlockSpec((blk, qw), lambda b, j, s: (b * nb + j, 0)),
        ),
        out_shape=jax.ShapeDtypeStruct((t, qw), BF16),
        compiler_params=_cparams(("parallel", "arbitrary"), VMEM_LIMIT),
        name="swa_prompt",
    )(sinks.astype(F32), proj, kv, kv)


def _swa_sample_kernel(q_ref, ck_ref, cv_ref, new_ref, sink_ref, slope_ref, o_ref, ko_ref, vo_ref, *, cfg, group, l):
    w = ck_ref.shape[1]
    kvw = cfg.kv_heads * cfg.dh
    grp = cfg.q_heads // cfg.kv_heads
    rows = l * grp
    scale = cfg.dh ** -0.5
    r_i = lax.broadcasted_iota(I32, (rows, w + l), 0)
    m_i = lax.broadcasted_iota(I32, (rows, w + l), 1)
    dist = lax.shift_right_logical(r_i, int(math.log2(grp))) + w - m_i
    ok = (dist >= 0) & (dist < cfg.window)
    distf = dist.astype(F32)
    negmask = jnp.where(ok, 0.0, NEG_BIG)
    for s in range(group):
        new = new_ref[s]
        k_all = jnp.concatenate([ck_ref[s], new[:, :kvw]], axis=0)
        v_all = jnp.concatenate([cv_ref[s], new[:, kvw:]], axis=0)
        ko_ref[s] = k_all[l:]
        vo_ref[s] = v_all[l:]
        for g in range(cfg.kv_heads):
            slab = slice((g // 2) * LANES, (g // 2 + 1) * LANES)
            sc = lax.dot_general(q_ref[s, g], k_all[:, slab], (((1,), (1,)), ((), ())),
                                 preferred_element_type=F32)
            sc = sc * scale - slope_ref[g] * distf + negmask
            pr, inv = _softmax_sink(sc, sink_ref[g])
            o_ref[s, g] = jnp.dot(pr, v_all[:, slab], preferred_element_type=F32) * inv


def _swa_sample(cfg, q_pad, cache_k, cache_v, kv_new, sink_rows, slope_rows, batch, l, group=4):
    w = cache_k.shape[1]
    kvw = cfg.kv_heads * cfg.dh
    grp = cfg.q_heads // cfg.kv_heads
    rows = l * grp
    kern = functools.partial(_swa_sample_kernel, cfg=cfg, group=group, l=l)
    return pl.pallas_call(
        kern,
        grid=(batch // group,),
        in_specs=[pl.BlockSpec((group, cfg.kv_heads, rows, LANES), lambda i: (i, 0, 0, 0)),
                  pl.BlockSpec((group, w, kvw), lambda i: (i, 0, 0)),
                  pl.BlockSpec((group, w, kvw), lambda i: (i, 0, 0)),
                  pl.BlockSpec((group, l, 2 * kvw), lambda i: (i, 0, 0)),
                  pl.BlockSpec((cfg.kv_heads, rows, 1), lambda i: (0, 0, 0)),
                  pl.BlockSpec((cfg.kv_heads, rows, 1), lambda i: (0, 0, 0))],
        out_specs=[pl.BlockSpec((group, cfg.kv_heads, rows, LANES), lambda i: (i, 0, 0, 0)),
                   pl.BlockSpec((group, w, kvw), lambda i: (i, 0, 0)),
                   pl.BlockSpec((group, w, kvw), lambda i: (i, 0, 0))],
        out_shape=[jax.ShapeDtypeStruct((batch, cfg.kv_heads, rows, LANES), F32),
                   jax.ShapeDtypeStruct((batch, w, kvw), F32),
                   jax.ShapeDtypeStruct((batch, w, kvw), F32)],
        compiler_params=_cparams(("parallel",), VMEM_LIMIT),
        name="swa_sample",
    )(q_pad, cache_k, cache_v, kv_new, sink_rows, slope_rows)


def _merge_kernel(a_ref, o_ref, ga_ref, gb_ref, x_ref, g1_ref, sh_ref, sc_ref, ng_ref, wr_ref, ws_ref, wo_ref,
                  wrt_ref, brt_ref, x1_ref, h2_ref, ti_ref, tp_ref, acc_scr, *, per_token, top_k):
    kk = pl.program_id(1)
    a = jnp.dot(a_ref[...].astype(BF16), wr_ref[...], preferred_element_type=F32)
    b = jnp.dot(o_ref[...].astype(BF16), ws_ref[...], preferred_element_type=F32)
    m = _sigmoid(ga_ref[...].astype(F32)) * a + _sigmoid(gb_ref[...].astype(F32)) * b
    contrib = jnp.dot(m.astype(BF16), wo_ref[...], preferred_element_type=F32)

    @pl.when(kk == 0)
    def _():
        acc_scr[...] = contrib

    @pl.when(kk > 0)
    def _():
        acc_scr[...] += contrib

    @pl.when(kk == pl.num_programs(1) - 1)
    def _():
        x1 = x_ref[...] + _mod_rows(g1_ref, per_token) * acc_scr[...]
        x1_ref[...] = x1
        h2 = _modulated_norm(x1, ng_ref[...], _mod_rows(sh_ref, per_token), _mod_rows(sc_ref, per_token))
        h2_ref[...] = h2
        logits = jnp.dot(h2.astype(BF16), wrt_ref[...], preferred_element_type=F32) + brt_ref[...]
        tm, ne = logits.shape
        lane = lax.broadcasted_iota(I32, (tm, ne), 1)
        lane_out = lax.broadcasted_iota(I32, (tm, LANES), 1)
        vals, idxs = [], []
        cur = logits
        for _ in range(top_k):
            mx = jnp.max(cur, axis=-1, keepdims=True)
            ix = jnp.min(jnp.where(cur == mx, lane, ne), axis=-1, keepdims=True)
            vals.append(mx)
            idxs.append(ix)
            cur = jnp.where(lane == ix, -jnp.inf, cur)
        es = [jnp.exp(v - vals[0]) for v in vals]
        inv = 1.0 / functools.reduce(lambda p, q: p + q, es)
        ti = jnp.zeros((tm, LANES), I32)
        tp = jnp.zeros((tm, LANES), F32)
        for r in range(top_k):
            ti = jnp.where(lane_out == r, idxs[r], ti)
            tp = jnp.where(lane_out == r, es[r] * inv, tp)
        ti_ref[...] = ti
        tp_ref[...] = tp


def _merge_out(cfg, a_in, o_swa, proj, x, g1, sh2, sc2, norm_g, wr_b, ws_b, wo_b, wrt_b, b_router, per_token, tm,
               tiles_per_seq, tk=256):
    t, d = x.shape
    offs, _ = _offsets(cfg)
    ka = a_in.shape[1]
    ko = o_swa.shape[1]
    assert d % tk == 0 and offs["ga"] % tk == 0 and offs["gb"] % tk == 0
    ga_blk, gb_blk = offs["ga"] // tk, offs["gb"] // tk
    ne = cfg.n_experts
    kern = functools.partial(_merge_kernel, per_token=per_token, top_k=cfg.top_k)
    mspec = lambda: _mod_spec(per_token, tm, d, tiles_per_seq)
    return pl.pallas_call(
        kern,
        grid=(t // tm, d // tk),
        in_specs=[pl.BlockSpec((tm, ka), lambda i, k: (i, 0)),
                  pl.BlockSpec((tm, ko), lambda i, k: (i, 0)),
                  pl.BlockSpec((tm, tk), lambda i, k: (i, ga_blk + k)),
                  pl.BlockSpec((tm, tk), lambda i, k: (i, gb_blk + k)),
                  pl.BlockSpec((tm, d), lambda i, k: (i, 0)),
                  mspec(), mspec(), mspec(),
                  pl.BlockSpec((1, d), lambda i, k: (0, 0)),
                  pl.BlockSpec((ka, tk), lambda i, k: (0, k)),
                  pl.BlockSpec((ko, tk), lambda i, k: (0, k)),
                  pl.BlockSpec((tk, d), lambda i, k: (k, 0)),
                  pl.BlockSpec((d, ne), lambda i, k: (0, 0)),
                  pl.BlockSpec((1, ne), lambda i, k: (0, 0))],
        out_specs=[pl.BlockSpec((tm, d), lambda i, k: (i, 0)),
                   pl.BlockSpec((tm, d), lambda i, k: (i, 0)),
                   pl.BlockSpec((tm, LANES), lambda i, k: (i, 0)),
                   pl.BlockSpec((tm, LANES), lambda i, k: (i, 0))],
        out_shape=[jax.ShapeDtypeStruct((t, d), F32), jax.ShapeDtypeStruct((t, d), F32),
                   jax.ShapeDtypeStruct((t, LANES), I32), jax.ShapeDtypeStruct((t, LANES), F32)],
        scratch_shapes=[pltpu.VMEM((tm, d), F32)],
        compiler_params=_cparams(("parallel", "arbitrary"), VMEM_LIMIT),
        name="merge_out",
    )(a_in, o_swa, proj, proj, x, g1, sh2, sc2, norm_g.reshape(1, d), wr_b, ws_b, wo_b, wrt_b,
      b_router.reshape(1, ne))


def _moe_kernel(ce_ref, cn_ref, idx_ref, h2_ref, wg_ref, wu_ref, bg_ref, bu_ref, wd_ref, bd_ref, y_ref,
                xbuf, xb, sem):
    c = pl.program_id(0)
    f = pl.program_id(1)
    nsub = cn_ref[c]

    def row_copy(r):
        tok = idx_ref[c * MOE_CHUNK + r]
        return pltpu.make_async_copy(h2_ref.at[pl.ds(tok, 1)], xbuf.at[pl.ds(r, 1)], sem)

    @pl.when(f == 0)
    def _():
        nrows = nsub * MOE_SUB

        def issue(r, carry):
            row_copy(r).start()
            return carry

        def wait(r, carry):
            row_copy(r).wait()
            return carry

        lax.fori_loop(0, nrows, issue, 0)
        lax.fori_loop(0, nrows, wait, 0)

        def cast(s, carry):
            r0 = pl.multiple_of(s * MOE_SUB, MOE_SUB)
            xb[pl.ds(r0, MOE_SUB), :] = xbuf[pl.ds(r0, MOE_SUB), :].astype(BF16)
            return carry

        lax.fori_loop(0, nsub, cast, 0)

    def block(s, carry):
        r0 = pl.multiple_of(s * MOE_SUB, MOE_SUB)
        x = xb[pl.ds(r0, MOE_SUB), :]
        g = jnp.dot(x, wg_ref[0], preferred_element_type=F32) + bg_ref[0]
        u = jnp.dot(x, wu_ref[0], preferred_element_type=F32) + bu_ref[0]
        gate = jnp.minimum(g, SWIGLU_LIMIT)
        up = jnp.clip(u, -SWIGLU_LIMIT, SWIGLU_LIMIT)
        act = (up + 1.0) * gate * _sigmoid(SWIGLU_ALPHA * gate)
        contrib = jnp.dot(act.astype(BF16), wd_ref[0], preferred_element_type=F32)

        @pl.when(f == 0)
        def _():
            y_ref[pl.ds(r0, MOE_SUB), :] = contrib + bd_ref[0]

        @pl.when(f > 0)
        def _():
            y_ref[pl.ds(r0, MOE_SUB), :] += contrib

        return carry

    lax.fori_loop(0, nsub, block, 0)

    @pl.when(f == 0)
    def _():
        def zero(s, carry):
            r0 = pl.multiple_of(s * MOE_SUB, MOE_SUB)
            y_ref[pl.ds(r0, MOE_SUB), :] = jnp.zeros((MOE_SUB, y_ref.shape[1]), F32)
            return carry

        lax.fori_loop(nsub, MOE_CHUNK // MOE_SUB, zero, 0)


def _moe(cfg, h2, chunk_expert, chunk_nsub, row_idx, wgu_b, b_gate_up, wd_b, b_down, tf=512):
    t, d = h2.shape
    ne, _, f2 = wgu_b.shape
    dff = f2 // 2
    nch = chunk_expert.shape[0]
    nf = dff // tf
    return pl.pallas_call(
        _moe_kernel,
        grid_spec=pltpu.PrefetchScalarGridSpec(
            num_scalar_prefetch=3,
            grid=(nch, nf),
            in_specs=[pl.BlockSpec(memory_space=pl.ANY),
                      pl.BlockSpec((1, d, tf), lambda c, f, ce, cn, ix: (ce[c], 0, f)),
                      pl.BlockSpec((1, d, tf), lambda c, f, ce, cn, ix: (ce[c], 0, nf + f)),
                      pl.BlockSpec((1, 1, tf), lambda c, f, ce, cn, ix: (ce[c], 0, f)),
                      pl.BlockSpec((1, 1, tf), lambda c, f, ce, cn, ix: (ce[c], 0, nf + f)),
                      pl.BlockSpec((1, tf, d), lambda c, f, ce, cn, ix: (ce[c], f, 0)),
                      pl.BlockSpec((1, 1, d), lambda c, f, ce, cn, ix: (ce[c], 0, 0))],
            out_specs=pl.BlockSpec((MOE_CHUNK, d), lambda c, f, ce, cn, ix: (c, 0)),
            scratch_shapes=[pltpu.VMEM((MOE_CHUNK, d), F32), pltpu.VMEM((MOE_CHUNK, d), BF16),
                            pltpu.SemaphoreType.DMA],
        ),
        out_shape=jax.ShapeDtypeStruct((nch * MOE_CHUNK, d), F32),
        compiler_params=_cparams(("arbitrary", "arbitrary"), VMEM_LIMIT),
        name="moe_mlp",
    )(chunk_expert, chunk_nsub, row_idx, h2, wgu_b, wgu_b, b_gate_up.reshape(ne, 1, f2),
      b_gate_up.reshape(ne, 1, f2), wd_b, b_down.reshape(ne, 1, d))


def _moe_plan(cfg, top_idx, n_chunks):
    t, k = top_idx.shape
    ne = cfg.n_experts
    flat_e = top_idx.reshape(-1)
    order = jnp.argsort(flat_e, stable=True)
    e_sorted = flat_e[order]
    sizes = jnp.bincount(flat_e, length=ne).astype(I32)
    starts = jnp.cumsum(sizes) - sizes
    n_ch = (sizes + MOE_CHUNK - 1) // MOE_CHUNK
    ch_end = jnp.cumsum(n_ch)
    ch_start = ch_end - n_ch
    total = ch_end[-1]
    cidx = jnp.arange(n_chunks, dtype=I32)
    last = jnp.maximum(total - 1, 0)
    c_eff = jnp.minimum(cidx, last)
    chunk_expert = jnp.minimum(jnp.searchsorted(ch_end, c_eff, side="right"), ne - 1).astype(I32)
    rows_in = jnp.clip(sizes[chunk_expert] - (c_eff - ch_start[chunk_expert]) * MOE_CHUNK, 0, MOE_CHUNK)
    rows_in = jnp.where(cidx < total, rows_in, 0)
    chunk_nsub = ((rows_in + MOE_SUB - 1) // MOE_SUB).astype(I32)
    rank = jnp.arange(t * k, dtype=I32) - starts[e_sorted]
    pos_sorted = ch_start[e_sorted] * MOE_CHUNK + rank
    row_idx = jnp.zeros((n_chunks * MOE_CHUNK,), I32).at[pos_sorted].set((order // k).astype(I32))
    pos = jnp.zeros((t * k,), I32).at[order].set(pos_sorted.astype(I32)).reshape(t, k)
    return chunk_expert, chunk_nsub, row_idx, pos


def _combine_kernel(pos_ref, y_ref, tp_ref, x1_ref, g2_ref, gf_ref, o_ref, buf, sem, *, per_token, top_k, tt, base):
    i = pl.program_id(0)

    def row_copy(tok, k):
        src = pos_ref[(base + i * tt + tok) * top_k + k]
        return pltpu.make_async_copy(y_ref.at[pl.ds(src, 1)], buf.at[k, pl.ds(tok, 1)], sem)

    def issue(tok, carry):
        for k in range(top_k):
            row_copy(tok, k).start()
        return carry

    def wait(tok, carry):
        for k in range(top_k):
            row_copy(tok, k).wait()
        return carry

    lax.fori_loop(0, tt, issue, 0)
    lax.fori_loop(0, tt, wait, 0)
    tp = tp_ref[...]
    ffn = tp[:, 0:1] * buf[0]
    for k in range(1, top_k):
        ffn = ffn + tp[:, k:k + 1] * buf[k]
    x2 = x1_ref[...] + _mod_rows(g2_ref, per_token) * ffn
    o_ref[...] = x2 * lax.rsqrt(jnp.mean(x2 * x2, axis=-1, keepdims=True) + NORM_EPS) * gf_ref[...]


def _combine(cfg, pos_flat, y_pad, top_p, x1, g2, g_final, per_token, tt, tiles_per_seq, base):
    t, d = x1.shape
    kern = functools.partial(_combine_kernel, per_token=per_token, top_k=cfg.top_k, tt=tt, base=base)
    return pl.pallas_call(
        kern,
        grid_spec=pltpu.PrefetchScalarGridSpec(
            num_scalar_prefetch=1,
            grid=(t // tt,),
            in_specs=[pl.BlockSpec(memory_space=pl.ANY),
                      pl.BlockSpec((tt, LANES), lambda i, p: (i, 0)),
                      pl.BlockSpec((tt, d), lambda i, p: (i, 0)),
                      _mod_spec(per_token, tt, d, tiles_per_seq),
                      pl.BlockSpec((1, d), lambda i, p: (0, 0))],
            out_specs=pl.BlockSpec((tt, d), lambda i, p: (i, 0)),
            scratch_shapes=[pltpu.VMEM((cfg.top_k, tt, d), F32), pltpu.SemaphoreType.DMA],
        ),
        out_shape=jax.ShapeDtypeStruct((t, d), F32),
        compiler_params=_cparams(("arbitrary",), VMEM_LIMIT),
        name="combine",
    )(pos_flat, y_pad, top_p, x1, g2, g_final.reshape(1, d))


def _forward(cfg, x_prompt, x_sample, state_ret, cache_swa_k, cache_swa_v, c_prompt, c_sample, w_ada, b_ada,
             norm_mix_g, w_in, ret_gn_g, w_ret_o, swa_sinks, w_swa_o, w_out, norm_ffn_g, w_router, b_router,
             w_gate_up, b_gate_up, w_down, b_down, norm_final_g, tm_prompt=512):
    assert w_ada.shape[0] == 1, "one layer"
    bp, seq, d = x_prompt.shape
    bs, ls, _ = x_sample.shape
    tp_tok, ts_tok = bp * seq, bs * ls
    offs, d_in = _offsets(cfg)
    kvw = cfg.kv_heads * cfg.dh
    grp = cfg.q_heads // cfg.kv_heads
    tm_p = min(tm_prompt, seq)
    tm_s = min(256, ts_tok)
    assert seq % tm_p == 0 and ts_tok % tm_s == 0 and LANES % cfg.dh == 0

    w_in_b = w_in[0].astype(BF16)
    wr_b = w_ret_o[0].astype(BF16)
    ws_b = w_swa_o[0].astype(BF16)
    wo_b = w_out[0].astype(BF16)
    wrt_b = w_router[0].astype(BF16)
    wgu_b = w_gate_up[0].astype(BF16)
    wd_b = w_down[0].astype(BF16)

    mod = _adaln(jnp.concatenate([c_prompt, c_sample], axis=0), w_ada[0], b_ada[0]).reshape(bp + bs, N_MOD, d)
    mod_p = [mod[:bp, i].reshape(bp, 1, d) for i in range(N_MOD)]
    mod_s = [jnp.repeat(mod[bp:, i], ls, axis=0) for i in range(N_MOD)]
    xp = x_prompt.reshape(tp_tok, d)
    xs = x_sample.reshape(ts_tok, d)
    tps = seq // tm_p

    proj_p, kv_p = _inproj(cfg, xp, mod_p[0], mod_p[1], norm_mix_g[0], w_in_b, False, tm_p, tps)
    a_p, s_p = _ret_prompt(cfg, proj_p, ret_gn_g[0], bp, seq)
    o_p = _swa_prompt(cfg, proj_p, kv_p, swa_sinks[0], bp, seq)
    x1_p, h2_p, ti_p, tpr_p = _merge_out(cfg, a_p, o_p, proj_p, xp, mod_p[2], mod_p[3], mod_p[4], norm_ffn_g[0],
                                         wr_b, ws_b, wo_b, wrt_b, b_router[0], False, tm_p, tps)

    proj_s, kv_s = _inproj(cfg, xs, mod_s[0], mod_s[1], norm_mix_g[0], w_in_b, True, tm_s, 1)
    a_s, s_s = _ret_sample(cfg, proj_s, ret_gn_g[0], state_ret[0], bs, ls)
    w = cache_swa_k.shape[2]
    sq = proj_s[:, offs["sq"]:offs["sq"] + cfg.q_heads * cfg.dh].astype(F32)
    sq = sq.reshape(bs, ls, cfg.kv_heads, grp, cfg.dh).transpose(0, 2, 1, 3, 4).reshape(bs, cfg.kv_heads, ls * grp,
                                                                                       cfg.dh)
    halves = LANES // cfg.dh
    q_pad = jnp.zeros((bs, cfg.kv_heads, ls * grp, halves, cfg.dh), F32)
    for g in range(cfg.kv_heads):
        q_pad = q_pad.at[:, g, :, g % halves, :].set(sq[:, g])
    q_pad = q_pad.reshape(bs, cfg.kv_heads, ls * grp, LANES)
    head_of_row = (jnp.arange(cfg.kv_heads)[:, None] * grp + jnp.arange(ls * grp)[None, :] % grp)
    sink_rows = swa_sinks[0].astype(F32)[head_of_row][..., None]
    slope_rows = (2.0 ** (-8.0 * (head_of_row + 1).astype(F32) / cfg.q_heads))[..., None]
    o_s4, k_s, v_s = _swa_sample(cfg, q_pad, cache_swa_k[0].reshape(bs, w, kvw), cache_swa_v[0].reshape(bs, w, kvw),
                                 kv_s.reshape(bs, ls, 2 * kvw), sink_rows, slope_rows, bs, ls)
    o_s4 = o_s4.reshape(bs, cfg.kv_heads, ls, grp, halves, cfg.dh)
    o_s = jnp.stack([o_s4[:, g, :, :, g % halves, :] for g in range(cfg.kv_heads)], axis=2)
    o_s = o_s.reshape(ts_tok, cfg.q_heads * cfg.dh)
    x1_s, h2_s, ti_s, tpr_s = _merge_out(cfg, a_s, o_s, proj_s, xs, mod_s[2], mod_s[3], mod_s[4], norm_ffn_g[0],
                                         wr_b, ws_b, wo_b, wrt_b, b_router[0], True, tm_s, 1)

    t_all = tp_tok + ts_tok
    h2_all = jnp.concatenate([h2_p, h2_s], axis=0)
    top_idx = jnp.concatenate([ti_p[:, :cfg.top_k], ti_s[:, :cfg.top_k]], axis=0)
    n_chunks = (t_all * cfg.top_k) // MOE_CHUNK + cfg.n_experts
    chunk_expert, chunk_nsub, row_idx, pos = _moe_plan(cfg, top_idx, n_chunks)
    y_pad = _moe(cfg, h2_all, chunk_expert, chunk_nsub, row_idx, wgu_b, b_gate_up[0], wd_b, b_down[0],
                 tf=min(512, cfg.d_ff))
    pos_flat = pos.reshape(-1)
    tt_p = min(256, seq)
    tt_s = min(256, ts_tok)
    y_p = _combine(cfg, pos_flat, y_pad, tpr_p, x1_p, mod_p[5], norm_final_g, False, tt_p, seq // tt_p, 0)
    y_s = _combine(cfg, pos_flat, y_pad, tpr_s, x1_s, mod_s[5], norm_final_g, True, tt_s, 1, tp_tok)

    wk = min(cfg.window, seq)
    kv_p3 = kv_p.reshape(bp, seq, 2 * kvw)
    k_keep_p = kv_p3[:, seq - wk:, :kvw].reshape(1, bp, wk, cfg.kv_heads, cfg.dh)
    v_keep_p = kv_p3[:, seq - wk:, kvw:].reshape(1, bp, wk, cfg.kv_heads, cfg.dh)
    return (y_p.reshape(bp, seq, d), y_s.reshape(bs, ls, d), s_p[None], k_keep_p, v_keep_p, s_s[None],
            k_s.reshape(1, bs, w, cfg.kv_heads, cfg.dh), v_s.reshape(1, bs, w, cfg.kv_heads, cfg.dh))


def kernel(x_prompt, x_sample, state_ret, cache_swa_k, cache_swa_v, c_prompt, c_sample, w_ada, b_ada, norm_mix_g,
           w_in, ret_gn_g, w_ret_o, swa_sinks, w_swa_o, w_out, norm_ffn_g, w_router, b_router, w_gate_up, b_gate_up,
           w_down, b_down, norm_final_g):
    return _forward(Cfg(), x_prompt, x_sample, state_ret, cache_swa_k, cache_swa_v, c_prompt, c_sample, w_ada, b_ada,
                    norm_mix_g, w_in, ret_gn_g, w_ret_o, swa_sinks, w_swa_o, w_out, norm_ffn_g, w_router, b_router,
                    w_gate_up, b_gate_up, w_down, b_down, norm_final_g)
```
